```python
import jax, jax.numpy as jnp
from jax import lax
import numpy as np

D_MODEL = 2048
BATCH = 2
SEQ = 4096
DEPTH = 2
DEC_BATCH = 2
DEC_SEQ = 16384
PAST_LEN = 128

N_MEM = 256
MLA_HEADS = 8
MLA_NOPE = 128
MLA_ROPE = 64
MLA_V = 128
Q_LORA = 512
KV_LORA = 256
HG_HEADS = 8
HG_DK = 128
HG_DV = 128
CHUNK = 64
MIX_WIDTH = MLA_HEADS * MLA_V + HG_HEADS * HG_DV
OFF_CQ = 0
OFF_CKV = OFF_CQ + Q_LORA
OFF_KR = OFF_CKV + KV_LORA
OFF_HQ = OFF_KR + MLA_ROPE
OFF_HFF = OFF_HQ + HG_HEADS * HG_DK
OFF_HFB = OFF_HFF + HG_HEADS * HG_DK
OFF_HI = OFF_HFB + HG_HEADS * HG_DK
OFF_HG = OFF_HI + HG_HEADS * HG_DV
IN_COLS = OFF_HG + HG_HEADS * HG_DV
XATTN_HEADS = 4
XATTN_DH = D_MODEL // XATTN_HEADS
D_FF = 4 * D_MODEL
Q_BLOCK = 128
ROPE_THETA = 10000.0
EPS = 1e-6

kernel_name = "hybrid_mla_hgrn2_bidir_encoder"


def rms_norm(x, g):
    xf = x.astype(jnp.float32)
    y = xf * lax.rsqrt(jnp.mean(xf * xf, axis=-1, keepdims=True) + EPS)
    return (y * g.astype(jnp.float32)).astype(x.dtype)


def apply_rope(x, seq_len):
    half = x.shape[-1] // 2
    pos = jnp.arange(seq_len, dtype=jnp.float32)
    inv_freq = ROPE_THETA ** (-jnp.arange(half, dtype=jnp.float32) / half)
    ang = pos[:, None] * inv_freq[None, :]
    cos = jnp.cos(ang)[None, :, None, :]
    sin = jnp.sin(ang)[None, :, None, :]
    xf = x.astype(jnp.float32)
    x1, x2 = xf[..., :half], xf[..., half:]
    out = jnp.concatenate([x1 * cos - x2 * sin, x2 * cos + x1 * sin], axis=-1)
    return out.astype(x.dtype)


def block_attention(q, k, v, scale):
    b, s, h, dq = q.shape
    nb = s // Q_BLOCK
    qb = q.reshape(b, nb, Q_BLOCK, h, dq).transpose(1, 0, 2, 3, 4)

    def one_block(qblk):
        sc = jnp.einsum('bqhd,bkhd->bhqk', qblk, k).astype(jnp.float32) * scale
        p = jax.nn.softmax(sc, axis=-1).astype(v.dtype)
        return jnp.einsum('bhqk,bkhd->bqhd', p, v)

    o = lax.map(one_block, qb)
    return o.transpose(1, 0, 2, 3, 4).reshape(b, s, h, v.shape[-1])


def mla_group(proj, q_norm, w_uq, kv_norm, w_ukv):
    b, s, _ = proj.shape
    c_q = proj[..., OFF_CQ:OFF_CQ + Q_LORA]
    c_kv = proj[..., OFF_CKV:OFF_CKV + KV_LORA]
    k_r = proj[..., OFF_KR:OFF_KR + MLA_ROPE]
    q = (rms_norm(c_q, q_norm) @ w_uq).reshape(b, s, MLA_HEADS, MLA_NOPE + MLA_ROPE)
    kv = (rms_norm(c_kv, kv_norm) @ w_ukv).reshape(b, s, MLA_HEADS, MLA_NOPE + MLA_V)
    q_nope, q_rope = q[..., :MLA_NOPE], q[..., MLA_NOPE:]
    k_nope, v = kv[..., :MLA_NOPE], kv[..., MLA_NOPE:]
    q_rope = apply_rope(q_rope, s)
    k_rope = apply_rope(k_r[:, :, None, :], s)
    q = jnp.concatenate([q_nope, q_rope], axis=-1)
    k = jnp.concatenate([k_nope, jnp.broadcast_to(k_rope, (b, s, MLA_HEADS, MLA_ROPE))], axis=-1)
    o = block_attention(q, k, v, (MLA_NOPE + MLA_ROPE) ** -0.5)
    return o.reshape(b, s, MLA_HEADS * MLA_V)


def hgrn_direction(q, k, v, logf):
    b, s, h, dk = q.shape
    dv = v.shape[-1]
    n = s // CHUNK

    def chunks(t):
        return t.reshape(b, n, CHUNK, h, t.shape[-1]).transpose(1, 0, 3, 2, 4)

    lower = jnp.tril(jnp.ones((CHUNK, CHUNK), dtype=bool))[None, None, :, :, None]

    def step(state, inp):
        qc, kc, vc, lc = inp
        bcum = jnp.cumsum(lc, axis=2)
        inter = jnp.einsum('bhtk,bhkv->bhtv', qc * jnp.exp(bcum), state)
        rel = bcum[:, :, :, None, :] - bcum[:, :, None, :, :]
        decay = jnp.exp(jnp.where(lower, rel, -jnp.inf))
        scores = jnp.einsum('bhtk,bhsk,bhtsk->bhts', qc, kc, decay)
        intra = jnp.einsum('bhts,bhsv->bhtv', scores, vc)
        btot = bcum[:, :, -1:, :]
        new_state = jnp.exp(btot[:, :, 0, :])[..., None] * state + jnp.einsum(
            'bhsk,bhsv->bhkv', kc * jnp.exp(btot - bcum), vc)
        return new_state, inter + intra

    init = jnp.zeros((b, h, dk, dv), jnp.float32)
    _, o = lax.scan(step, init, (chunks(q), chunks(k), chunks(v), chunks(logf)))
    return o.transpose(1, 0, 3, 2, 4).reshape(b, s, h, dv)


def hgrn2_group(proj, lb, out_norm):
    b, s, _ = proj.shape
    f32 = jnp.float32
    q = jax.nn.silu(proj[..., OFF_HQ:OFF_HFF].astype(f32)) * (HG_DK ** -0.5)
    q = q.reshape(b, s, HG_HEADS, HG_DK)
    v = proj[..., OFF_HI:OFF_HG].astype(f32).reshape(b, s, HG_HEADS, HG_DV)
    g = proj[..., OFF_HG:IN_COLS].astype(f32).reshape(b, s, HG_HEADS, HG_DV)

    def gates(z, lbd):
        z = z.astype(f32).reshape(b, s, HG_HEADS, HG_DK)
        lbd = lbd.astype(f32).reshape(HG_HEADS, HG_DK)
        logf = jnp.logaddexp(jnp.log(lbd), jnp.log1p(-lbd) + jax.nn.log_sigmoid(z))
        k = (1.0 - lbd) * jax.nn.sigmoid(-z)
        return k, logf

    k_f, lf_f = gates(proj[..., OFF_HFF:OFF_HFB], lb[0])
    k_b, lf_b = gates(proj[..., OFF_HFB:OFF_HI], lb[1])
    o_f = hgrn_direction(q, k_f, v, lf_f)
    o_b = jnp.flip(hgrn_direction(jnp.flip(q, 1), jnp.flip(k_b, 1), jnp.flip(v, 1), jnp.flip(lf_b, 1)), 1)
    o = o_f + o_b
    o = o * lax.rsqrt(jnp.mean(o * o, axis=-1, keepdims=True) + EPS) * out_norm.astype(f32)
    o = o * jax.nn.silu(g)
    return o.reshape(b, s, HG_HEADS * HG_DV).astype(proj.dtype)


def memory_cross_attention(h, m, w_xq, w_xk, w_xv, w_xo):
    b, s, _ = h.shape
    nm = m.shape[1]
    q = (h @ w_xq).reshape(b, s, XATTN_HEADS, XATTN_DH)
    k = (m @ w_xk).reshape(b, nm, XATTN_HEADS, XATTN_DH)
    v = (m @ w_xv).reshape(b, nm, XATTN_HEADS, XATTN_DH)
    sc = jnp.einsum('bqhd,bkhd->bhqk', q, k).astype(jnp.float32) * (XATTN_DH ** -0.5)
    p = jax.nn.softmax(sc, axis=-1).astype(v.dtype)
    o = jnp.einsum('bhqk,bkhd->bqhd', p, v).reshape(b, s, D_MODEL)
    return o @ w_xo


def trunk(x, mem, mix_norm, w_in, q_norm, w_uq, kv_norm, w_ukv, hgrn_lb_logits, hgrn_out_norm,
          w_out, xattn_norm, mem_norm, w_xq, w_xk, w_xv, w_xo, ffn_norm, w_ffn1, w_ffn2, final_norm):
    lb_all = jnp.cumsum(jax.nn.softmax(hgrn_lb_logits.astype(jnp.float32), axis=0), axis=0)
    lb_all = lb_all - lb_all[0:1]
    for l in range(DEPTH):
        h = rms_norm(x, mix_norm[l])
        proj = h @ w_in[l]
        a_out = mla_group(proj, q_norm[l], w_uq[l], kv_norm[l], w_ukv[l])
        b_out = hgrn2_group(proj, lb_all[l], hgrn_out_norm[l])
        x = x + jnp.concatenate([a_out, b_out], axis=-1) @ w_out[l]
        h = rms_norm(x, xattn_norm[l])
        m = rms_norm(mem, mem_norm[l])
        x = x + memory_cross_attention(h, m, w_xq[l], w_xk[l], w_xv[l], w_xo[l])
        h = rms_norm(x, ffn_norm[l])
        u = jax.nn.relu(h @ w_ffn1[l])
        x = x + (u * u) @ w_ffn2[l]
    return rms_norm(x, final_norm)


def setup_inputs(seed: int = 0) -> dict:
    key = jax.random.key(seed)
    ks = jax.random.split(key, 24)
    f32 = jnp.float32

    def w(k, shape, fan_in):
        return jax.random.normal(k, shape, f32) * (fan_in ** -0.5)

    def gain(k, shape):
        return 1.0 + 0.01 * jax.random.normal(k, shape, f32)

    return {
        'x_prompt': jax.random.normal(ks[0], (BATCH, SEQ, D_MODEL), f32),
        'x_sample': jax.random.normal(ks[1], (DEC_BATCH, DEC_SEQ, D_MODEL), f32),
        'mem_prompt': jax.random.normal(ks[2], (BATCH, N_MEM, D_MODEL), f32),
        'mem_sample': jax.random.normal(ks[3], (DEC_BATCH, N_MEM, D_MODEL), f32),
        'mix_norm': gain(ks[4], (DEPTH, D_MODEL)),
        'w_in': w(ks[5], (DEPTH, D_MODEL, IN_COLS), D_MODEL),
        'q_norm': gain(ks[6], (DEPTH, Q_LORA)),
        'w_uq': w(ks[7], (DEPTH, Q_LORA, MLA_HEADS * (MLA_NOPE + MLA_ROPE)), Q_LORA),
        'kv_norm': gain(ks[8], (DEPTH, KV_LORA)),
        'w_ukv': w(ks[9], (DEPTH, KV_LORA, MLA_HEADS * (MLA_NOPE + MLA_V)), KV_LORA),
        'hgrn_lb_logits': 0.5 * jax.random.normal(ks[10], (DEPTH, 2, HG_HEADS * HG_DK), f32),
        'hgrn_out_norm': gain(ks[11], (DEPTH, HG_DV)),
        'w_out': w(ks[12], (DEPTH, MIX_WIDTH, D_MODEL), MIX_WIDTH),
        'xattn_norm': gain(ks[13], (DEPTH, D_MODEL)),
        'mem_norm': gain(ks[14], (DEPTH, D_MODEL)),
        'w_xq': w(ks[15], (DEPTH, D_MODEL, D_MODEL), D_MODEL),
        'w_xk': w(ks[16], (DEPTH, D_MODEL, D_MODEL), D_MODEL),
        'w_xv': w(ks[17], (DEPTH, D_MODEL, D_MODEL), D_MODEL),
        'w_xo': w(ks[18], (DEPTH, D_MODEL, D_MODEL), D_MODEL),
        'ffn_norm': gain(ks[19], (DEPTH, D_MODEL)),
        'w_ffn1': w(ks[20], (DEPTH, D_MODEL, D_FF), D_MODEL),
        'w_ffn2': w(ks[21], (DEPTH, D_FF, D_MODEL), D_FF),
        'final_norm': gain(ks[22], (D_MODEL,)),
    }


def reference(x_prompt, x_sample, mem_prompt, mem_sample, mix_norm, w_in, q_norm, w_uq, kv_norm,
              w_ukv, hgrn_lb_logits, hgrn_out_norm, w_out, xattn_norm, mem_norm, w_xq, w_xk, w_xv,
              w_xo, ffn_norm, w_ffn1, w_ffn2, final_norm):
    y_prompt = trunk(x_prompt, mem_prompt, mix_norm, w_in, q_norm, w_uq, kv_norm, w_ukv,
                     hgrn_lb_logits, hgrn_out_norm, w_out, xattn_norm, mem_norm, w_xq, w_xk, w_xv,
                     w_xo, ffn_norm, w_ffn1, w_ffn2, final_norm)
    y_sample = trunk(x_sample, mem_sample, mix_norm, w_in, q_norm, w_uq, kv_norm, w_ukv,
                     hgrn_lb_logits, hgrn_out_norm, w_out, xattn_norm, mem_norm, w_xq, w_xk, w_xv,
                     w_xo, ffn_norm, w_ffn1, w_ffn2, final_norm)
    return (y_prompt, y_sample)
```

```python
import functools
import math

import jax
import jax.numpy as jnp
from jax import lax
from jax.experimental import pallas as pl
from jax.experimental.pallas import tpu as pltpu

F32 = jnp.float32
BF16 = jnp.bfloat16

D_MODEL = 2048
MLA_HEADS = 8
MLA_NOPE = 128
MLA_ROPE = 64
MLA_V = 128
MLA_QK = MLA_NOPE + MLA_ROPE
Q_LORA = 512
KV_LORA = 256
HG_HEADS = 8
HG_DK = 128
HG_DV = 128
HG_WIDTH = HG_HEADS * HG_DK
XATTN_HEADS = 4
XATTN_DH = D_MODEL // XATTN_HEADS
ROPE_THETA = 10000.0
EPS = 1e-6
MLA_IN = Q_LORA + KV_LORA + MLA_ROPE

V7X_VMEM_BYTES = 64 * 1024 * 1024
V7X_LANES = 128
V7X_SUBLANES = 8
VMEM_LIMIT = V7X_VMEM_BYTES - 8 * 1024 * 1024

HG_CHUNK = 64
HG_DIAG = 8
HG_BLOCK = 512
PREP_ROWS = 512
ATTN_TQ = 512
ATTN_TK = PREP_ROWS

LOG2E = math.log2(math.e)

_NT = (((1,), (1,)), ((), ()))
_TN = (((0,), (0,)), ((), ()))


def _params(n_axes):
    return pltpu.CompilerParams(
        dimension_semantics=("arbitrary",) * n_axes, vmem_limit_bytes=VMEM_LIMIT)


def _rms(xf, g):
    ms = jnp.mean(xf * xf, axis=-1, keepdims=True)
    return xf * lax.rsqrt(ms + EPS) * g


def _dot(a, b):
    return jnp.dot(a, b, preferred_element_type=F32)


def _norm_mm_kernel(x_ref, g_ref, w_ref, o_ref, h_scr, *, act):
    @pl.when(pl.program_id(1) == 0)
    def _():
        h_scr[...] = _rms(x_ref[...], g_ref[...]).astype(BF16)

    acc = _dot(h_scr[...], w_ref[...])
    if act == "relu2":
        r = jnp.maximum(acc, 0.0)
        acc = r * r
    o_ref[...] = acc.astype(o_ref.dtype)


def norm_matmul(x, g, w, *, act=None, out_dtype=F32, tm=1024, tn=512):
    m, k = x.shape
    n = w.shape[1]
    tm = min(tm, m)
    tn = min(tn, n)
    assert m % tm == 0 and n % tn == 0
    return pl.pallas_call(
        functools.partial(_norm_mm_kernel, act=act),
        grid=(m // tm, n // tn),
        in_specs=[
            pl.BlockSpec((tm, k), lambda i, j: (i, 0)),
            pl.BlockSpec((1, k), lambda i, j: (0, 0)),
            pl.BlockSpec((k, tn), lambda i, j: (0, j)),
        ],
        out_specs=pl.BlockSpec((tm, tn), lambda i, j: (i, j)),
        out_shape=jax.ShapeDtypeStruct((m, n), out_dtype),
        scratch_shapes=[pltpu.VMEM((tm, k), BF16)],
        compiler_params=_params(2),
        name="norm_matmul",
    )(x, g.reshape(1, k), w)


def _mm_res_kernel(a_ref, w_ref, r_ref, o_ref):
    o_ref[...] = _dot(a_ref[...], w_ref[...]) + r_ref[...]


def matmul_residual(a, w, res, *, tm=1024, tn=256):
    m, k = a.shape
    n = w.shape[1]
    tm = min(tm, m)
    tn = min(tn, n)
    assert m % tm == 0 and n % tn == 0
    return pl.pallas_call(
        _mm_res_kernel,
        grid=(m // tm, n // tn),
        in_specs=[
            pl.BlockSpec((tm, k), lambda i, j: (i, 0)),
            pl.BlockSpec((k, tn), lambda i, j: (0, j)),
            pl.BlockSpec((tm, tn), lambda i, j: (i, j)),
        ],
        out_specs=pl.BlockSpec((tm, tn), lambda i, j: (i, j)),
        out_shape=jax.ShapeDtypeStruct((m, n), F32),
        compiler_params=_params(2),
        name="matmul_residual",
    )(a, w, res)


def _norm_kernel(x_ref, g_ref, o_ref):
    o_ref[...] = _rms(x_ref[...], g_ref[...])


def rmsnorm_rows(x, g, *, tm=1024):
    m, k = x.shape
    tm = min(tm, m)
    return pl.pallas_call(
        _norm_kernel,
        grid=(m // tm,),
        in_specs=[pl.BlockSpec((tm, k), lambda i: (i, 0)), pl.BlockSpec((1, k), lambda i: (0, 0))],
        out_specs=pl.BlockSpec((tm, k), lambda i: (i, 0)),
        out_shape=jax.ShapeDtypeStruct((m, k), F32),
        compiler_params=_params(1),
        name="final_norm",
    )(x, g.reshape(1, k))


def _mla_prep_kernel(x_ref, g_ref, wm_ref, qn_ref, wqt_ref, kvn_ref, wk_ref, wvt_ref,
                     cos_ref, sin_ref, cost_ref, sint_ref, qt_ref, k_ref, vt_ref):
    h = _rms(x_ref[0], g_ref[...]).astype(BF16)
    c = _dot(h, wm_ref[...])
    cq = c[:, :Q_LORA]
    ckv = c[:, Q_LORA:Q_LORA + KV_LORA]
    kr = c[:, Q_LORA + KV_LORA:MLA_IN]
    kr_rot = c[:, MLA_IN:MLA_IN + MLA_ROPE]
    cqn = _rms(cq, qn_ref[...]).astype(BF16)
    ckvn = _rms(ckv, kvn_ref[...]).astype(BF16)
    qt_all = lax.dot_general(wqt_ref[...], cqn, _NT, preferred_element_type=F32)
    k_all = _dot(ckvn, wk_ref[...])
    vt_all = lax.dot_general(wvt_ref[...], ckvn, _NT, preferred_element_type=F32)
    k_rope = (kr * cos_ref[...] + kr_rot * sin_ref[...]).astype(BF16)
    cos_t = cost_ref[...]
    sin_t = sint_ref[...]
    q_scale = (MLA_QK ** -0.5) * LOG2E
    per_head = MLA_NOPE + 2 * MLA_ROPE
    for hd in range(MLA_HEADS):
        base = hd * per_head
        q_nope = qt_all[base:base + MLA_NOPE]
        q_r = qt_all[base + MLA_NOPE:base + MLA_QK]
        q_rot = qt_all[base + MLA_QK:base + per_head]
        q_rope = q_r * cos_t + q_rot * sin_t
        qt_ref[0, hd, 0:MLA_NOPE, :] = (q_nope * q_scale).astype(BF16)
        qt_ref[0, hd, MLA_NOPE:MLA_QK, :] = (q_rope * q_scale).astype(BF16)
        k_ref[0, hd, :, 0:MLA_NOPE] = k_all[:, hd * MLA_NOPE:(hd + 1) * MLA_NOPE].astype(BF16)
        k_ref[0, hd, :, MLA_NOPE:MLA_QK] = k_rope
        vt_ref[0, hd, 0] = vt_all[hd * MLA_V:(hd + 1) * MLA_V].astype(BF16)


def mla_prep(x, g, wm, qn, wqt, kvn, wk, wvt, cos2, sin2, cos2t, sin2t):
    b, s, d = x.shape
    tm = min(PREP_ROWS, s)
    nck = s // tm
    full = lambda shp: pl.BlockSpec(shp, lambda bi, i: (0,) * len(shp))
    return pl.pallas_call(
        _mla_prep_kernel,
        grid=(b, nck),
        in_specs=[
            pl.BlockSpec((1, tm, d), lambda bi, i: (bi, i, 0)),
            full((1, d)),
            full(wm.shape),
            full((1, Q_LORA)),
            full(wqt.shape),
            full((1, KV_LORA)),
            full(wk.shape),
            full(wvt.shape),
            pl.BlockSpec((tm, MLA_ROPE), lambda bi, i: (i, 0)),
            pl.BlockSpec((tm, MLA_ROPE), lambda bi, i: (i, 0)),
            pl.BlockSpec((MLA_ROPE, tm), lambda bi, i: (0, i)),
            pl.BlockSpec((MLA_ROPE, tm), lambda bi, i: (0, i)),
        ],
        out_specs=[
            pl.BlockSpec((1, MLA_HEADS, MLA_QK, tm), lambda bi, i: (bi, 0, 0, i)),
            pl.BlockSpec((1, MLA_HEADS, tm, MLA_QK), lambda bi, i: (bi, 0, i, 0)),
            pl.BlockSpec((1, MLA_HEADS, 1, MLA_V, tm), lambda bi, i: (bi, 0, i, 0, 0)),
        ],
        out_shape=[
            jax.ShapeDtypeStruct((b, MLA_HEADS, MLA_QK, s), BF16),
            jax.ShapeDtypeStruct((b, MLA_HEADS, s, MLA_QK), BF16),
            jax.ShapeDtypeStruct((b, MLA_HEADS, nck, MLA_V, tm), BF16),
        ],
        compiler_params=_params(2),
        name="mla_prep",
    )(x, g.reshape(1, d), wm, qn.reshape(1, Q_LORA), wqt, kvn.reshape(1, KV_LORA), wk, wvt,
      cos2, sin2, cos2t, sin2t)


def _attn_kernel(qt_ref, k_ref, vt_ref, o_ref, *, nck, tk):
    qt = qt_ref[0, 0]
    tq = qt.shape[1]

    def body(c, carry):
        m, l, acc = carry
        kc = k_ref[0, 0, pl.ds(pl.multiple_of(c * tk, tk), tk), :]
        st = _dot(kc, qt)
        m_new = jnp.maximum(m, jnp.max(st, axis=0, keepdims=True))
        alpha = jnp.exp2(m - m_new)
        pt = jnp.exp2(st - m_new)
        l = alpha * l + jnp.sum(pt, axis=0, keepdims=True)
        acc = alpha * acc + _dot(vt_ref[0, 0, c], pt.astype(BF16))
        return m_new, l, acc

    m0 = jnp.full((1, tq), -jnp.inf, F32)
    l0 = jnp.zeros((1, tq), F32)
    acc0 = jnp.zeros((MLA_V, tq), F32)
    _, l, acc = lax.fori_loop(0, nck, body, (m0, l0, acc0))
    o_ref[0] = (acc / l).T.astype(o_ref.dtype)


def mla_attention(qt, k, vt):
    b, nh, _, s = qt.shape
    nck, tk = vt.shape[2], vt.shape[4]
    tq = min(ATTN_TQ, s)
    return pl.pallas_call(
        functools.partial(_attn_kernel, nck=nck, tk=tk),
        grid=(b, nh, s // tq),
        in_specs=[
            pl.BlockSpec((1, 1, MLA_QK, tq), lambda bi, h, i: (bi, h, 0, i)),
            pl.BlockSpec((1, 1, s, MLA_QK), lambda bi, h, i: (bi, h, 0, 0)),
            pl.BlockSpec((1, 1, nck, MLA_V, tk), lambda bi, h, i: (bi, h, 0, 0, 0)),
        ],
        out_specs=pl.BlockSpec((1, tq, MLA_V), lambda bi, h, i: (bi, i, h)),
        out_shape=jax.ShapeDtypeStruct((b, s, nh * MLA_V), BF16),
        compiler_params=_params(3),
        name="mla_attention",
    )(qt, k, vt)


def _split3(x):
    p1 = x.astype(BF16)
    r1 = x - p1.astype(F32)
    p2 = r1.astype(BF16)
    r2 = r1 - p2.astype(F32)
    return p1, p2, r2.astype(BF16)


def _hgrn_kernel(zq_ref, zf_ref, v_ref, lb_ref, o_ref, st_scr, *, reverse, n_chunks):
    c64 = HG_CHUNK
    nd = c64 // HG_DIAG

    @pl.when(pl.program_id(1) == 0)
    def _():
        st_scr[...] = jnp.zeros_like(st_scr)

    row = lax.broadcasted_iota(jnp.int32, (c64, c64), 0)
    col = lax.broadcasted_iota(jnp.int32, (c64, c64), 1)
    qpos, kpos = (col, row) if reverse else (row, col)
    tri = (kpos <= qpos).astype(BF16)
    level_masks = []
    for blk in (64, 32, 16):
        half = blk // 2
        level_masks.append((row // blk == col // blk) & (qpos % blk >= half) & (kpos % blk < half))
    dj = lax.broadcasted_iota(jnp.int32, (nd, HG_DIAG, c64), 0)
    dt = lax.broadcasted_iota(jnp.int32, (nd, HG_DIAG, c64), 1)
    dc = lax.broadcasted_iota(jnp.int32, (nd, HG_DIAG, c64), 2)

    def chunk(ci, carry):
        cidx = (n_chunks - 1 - ci) if reverse else ci
        r0 = pl.multiple_of(cidx * c64, c64)
        for hd in range(HG_HEADS):
            lanes = slice(hd * HG_DK, (hd + 1) * HG_DK)
            zq = zq_ref[0, pl.ds(r0, c64), lanes]
            z = zf_ref[0, pl.ds(r0, c64), lanes]
            v = v_ref[0, pl.ds(r0, c64), lanes]
            log_lb = lb_ref[0:1, lanes]
            log_1m_lb = lb_ref[1:2, lanes]
            one_m_lb = lb_ref[2:3, lanes]

            q = zq / (1.0 + jnp.exp(-zq)) * (HG_DK ** -0.5)
            e = jnp.exp(-jnp.abs(z))
            log_sig = jnp.minimum(z, 0.0) - jnp.log1p(e)
            cterm = log_1m_lb + log_sig
            delta = log_lb - cterm
            logf = jnp.maximum(log_lb, cterm) + jnp.log1p(jnp.exp(-jnp.abs(delta)))
            k = one_m_lb * jnp.where(z >= 0.0, e, 1.0) / (1.0 + e)

            p1, p2, p3 = _split3(logf)
            b = _dot(tri, p1) + _dot(tri, p2) + _dot(tri, p3)
            btot = jnp.sum(logf, axis=0, keepdims=True)

            st = st_scr[hd]
            qb = (q * jnp.exp(b)).astype(BF16)
            inter = lax.dot_general(qb, st.astype(BF16), _NT, preferred_element_type=F32)
            kd = (k * jnp.exp(btot - b)).astype(BF16)
            vb = v.astype(BF16)
            dst = lax.dot_general(vb, kd, _TN, preferred_element_type=F32)
            st_scr[hd] = st * jnp.exp(btot) + dst

            scores = jnp.zeros((c64, c64), F32)
            for blk, mask in zip((64, 32, 16), level_masks):
                half = blk // 2
                pieces = []
                for s0 in range(0, c64, blk):
                    rr = s0 + half if reverse else s0 + half - 1
                    pieces.append(jnp.broadcast_to(b[rr:rr + 1], (blk, HG_DK)))
                ref = jnp.concatenate(pieces, axis=0) if len(pieces) > 1 else pieces[0]
                ed = jnp.exp(-jnp.abs(b - ref))
                ql = (q * ed).astype(BF16)
                kl = (k * ed).astype(BF16)
                sl = lax.dot_general(ql, kl, _NT, preferred_element_type=F32)
                scores = scores + jnp.where(mask, sl, 0.0)
            q3 = q.reshape(nd, HG_DIAG, HG_DK)
            k3 = k.reshape(nd, HG_DIAG, HG_DK)
            b3 = b.reshape(nd, HG_DIAG, HG_DK)
            s3 = scores.reshape(nd, HG_DIAG, c64)
            for s_i in range(HG_DIAG):
                ks = jnp.broadcast_to(k3[:, s_i:s_i + 1, :], q3.shape)
                bs = jnp.broadcast_to(b3[:, s_i:s_i + 1, :], q3.shape)
                pair = q3 * ks * jnp.exp(jnp.minimum(b3 - bs, 0.0))
                rs = jnp.sum(pair, axis=-1, keepdims=True)
                ok = (dc == dj * HG_DIAG + s_i) & ((dt <= s_i) if reverse else (dt >= s_i))
                s3 = s3 + jnp.where(ok, rs, 0.0)
            scores = s3.reshape(c64, c64)
            intra = _dot(scores.astype(BF16), vb)
            o_ref[0, pl.ds(r0, c64), lanes] = inter + intra
        return carry

    lax.fori_loop(0, n_chunks, chunk, 0)


def hgrn_direction(proj, lb_rows, *, batch, seq, q_blk, f_blk, v_blk, reverse):
    tb = min(HG_BLOCK, seq)
    nb = seq // tb
    p3 = proj.reshape(batch, seq, proj.shape[-1])

    def spec(cb):
        if reverse:
            return pl.BlockSpec((1, tb, HG_WIDTH), lambda bi, n: (bi, nb - 1 - n, cb))
        return pl.BlockSpec((1, tb, HG_WIDTH), lambda bi, n: (bi, n, cb))

    return pl.pallas_call(
        functools.partial(_hgrn_kernel, reverse=reverse, n_chunks=tb // HG_CHUNK),
        grid=(batch, nb),
        in_specs=[spec(q_blk), spec(f_blk), spec(v_blk),
                  pl.BlockSpec((3, HG_WIDTH), lambda bi, n: (0, 0))],
        out_specs=spec(0),
        out_shape=jax.ShapeDtypeStruct((batch, seq, HG_WIDTH), F32),
        scratch_shapes=[pltpu.VMEM((HG_HEADS, HG_DV, HG_DK), F32)],
        compiler_params=_params(2),
        name="hgrn_bwd" if reverse else "hgrn_fwd",
    )(p3, p3, p3, lb_rows)


def _out_proj_kernel(a_ref, of_ref, ob_ref, g_ref, on_ref, wa_ref, wb_ref, r_ref, o_ref, b_scr):
    @pl.when(pl.program_id(1) == 0)
    def _():
        for hd in range(HG_HEADS):
            lanes = slice(hd * HG_DV, (hd + 1) * HG_DV)
            o = of_ref[:, lanes] + ob_ref[:, lanes]
            y = _rms(o, on_ref[...])
            g = g_ref[:, lanes]
            b_scr[:, lanes] = (y * (g / (1.0 + jnp.exp(-g)))).astype(BF16)

    o_ref[...] = _dot(a_ref[...], wa_ref[...]) + _dot(b_scr[...], wb_ref[...]) + r_ref[...]


def out_projection(a, o_f, o_b, proj_hg, g_blk, onorm, w_out, res, *, tm=512, tn=512):
    m = a.shape[0]
    ka = a.shape[1]
    n = w_out.shape[1]
    tm = min(tm, m)
    return pl.pallas_call(
        _out_proj_kernel,
        grid=(m // tm, n // tn),
        in_specs=[
            pl.BlockSpec((tm, ka), lambda i, j: (i, 0)),
            pl.BlockSpec((tm, HG_WIDTH), lambda i, j: (i, 0)),
            pl.BlockSpec((tm, HG_WIDTH), lambda i, j: (i, 0)),
            pl.BlockSpec((tm, HG_WIDTH), lambda i, j: (i, g_blk)),
            pl.BlockSpec((1, HG_DV), lambda i, j: (0, 0)),
            pl.BlockSpec((ka, tn), lambda i, j: (0, j)),
            pl.BlockSpec((HG_WIDTH, tn), lambda i, j: (1, j)),
            pl.BlockSpec((tm, tn), lambda i, j: (i, j)),
        ],
        out_specs=pl.BlockSpec((tm, tn), lambda i, j: (i, j)),
        out_shape=jax.ShapeDtypeStruct((m, n), F32),
        scratch_shapes=[pltpu.VMEM((tm, HG_WIDTH), BF16)],
        compiler_params=_params(2),
        name="out_projection",
    )(a, o_f, o_b, proj_hg, onorm.reshape(1, HG_DV), w_out, w_out, res)


def _xattn_kernel(x_ref, g_ref, wq_ref, k_ref, v_ref, o_ref, h_scr):
    @pl.when(pl.program_id(1) == 0)
    def _():
        h_scr[...] = _rms(x_ref[...], g_ref[...]).astype(BF16)

    q = _dot(h_scr[...], wq_ref[...]).astype(BF16)
    s = lax.dot_general(q, k_ref[0], _NT, preferred_element_type=F32) * (XATTN_DH ** -0.5)
    m = jnp.max(s, axis=-1, keepdims=True)
    p = jnp.exp(s - m)
    l = jnp.sum(p, axis=-1, keepdims=True)
    o_ref[...] = (_dot(p.astype(BF16), v_ref[0]) / l).astype(o_ref.dtype)


def cross_attention(x, g, wq, kmem, vmem, *, seq, tm=1024):
    m, d = x.shape
    tm = min(tm, seq)
    per_b = seq // tm
    nm = kmem.shape[1]
    return pl.pallas_call(
        _xattn_kernel,
        grid=(m // tm, XATTN_HEADS),
        in_specs=[
            pl.BlockSpec((tm, d), lambda i, h: (i, 0)),
            pl.BlockSpec((1, d), lambda i, h: (0, 0)),
            pl.BlockSpec((d, XATTN_DH), lambda i, h: (0, h)),
            pl.BlockSpec((1, nm, XATTN_DH), lambda i, h: (i // per_b, 0, h)),
            pl.BlockSpec((1, nm, XATTN_DH), lambda i, h: (i // per_b, 0, h)),
        ],
        out_specs=pl.BlockSpec((tm, XATTN_DH), lambda i, h: (i, h)),
        out_shape=jax.ShapeDtypeStruct((m, d), BF16),
        scratch_shapes=[pltpu.VMEM((tm, d), BF16)],
        compiler_params=_params(2),
        name="cross_attention",
    )(x, g.reshape(1, d), wq, kmem, vmem)


def _rot_cols(w):
    half = w.shape[-1] // 2
    return jnp.concatenate([-w[..., half:], w[..., :half]], axis=-1)


def _layer_weights(l, mix_norm, w_in, q_norm, w_uq, kv_norm, w_ukv, lb_all, hgrn_out_norm, w_out,
                   xattn_norm, mem_norm, w_xq, w_xk, w_xv, w_xo, ffn_norm, w_ffn1, w_ffn2):
    win = w_in[l]
    w_kr = win[:, Q_LORA + KV_LORA:MLA_IN]
    wm = jnp.concatenate([win[:, :MLA_IN], _rot_cols(w_kr)], axis=1).astype(BF16)
    wq = w_uq[l].reshape(Q_LORA, MLA_HEADS, MLA_QK)
    wq_rope = wq[..., MLA_NOPE:]
    wq_ext = jnp.concatenate([wq, _rot_cols(wq_rope)], axis=-1)
    wqt = wq_ext.reshape(Q_LORA, -1).T.astype(BF16)
    wkv = w_ukv[l].reshape(KV_LORA, MLA_HEADS, MLA_NOPE + MLA_V)
    wk = wkv[..., :MLA_NOPE].reshape(KV_LORA, -1).astype(BF16)
    wvt = wkv[..., MLA_NOPE:].reshape(KV_LORA, -1).T.astype(BF16)
    lb = lb_all[l]
    lb_rows = jnp.stack([jnp.log(lb), jnp.log1p(-lb), 1.0 - lb], axis=1)
    return dict(
        mix_norm=mix_norm[l], wm=wm, q_norm=q_norm[l], wqt=wqt, kv_norm=kv_norm[l], wk=wk, wvt=wvt,
        w_hg=win[:, MLA_IN:].astype(BF16), lb_rows=lb_rows, out_norm=hgrn_out_norm[l],
        w_out=w_out[l].astype(BF16), xattn_norm=xattn_norm[l], mem_norm=mem_norm[l],
        w_xq=w_xq[l].astype(BF16), w_xk=w_xk[l].astype(BF16), w_xv=w_xv[l].astype(BF16),
        w_xo=w_xo[l].astype(BF16), ffn_norm=ffn_norm[l], w_ffn1=w_ffn1[l].astype(BF16),
        w_ffn2=w_ffn2[l].astype(BF16))


def _rope_tables(seq):
    half = MLA_ROPE // 2
    pos = jnp.arange(seq, dtype=F32)
    inv_freq = ROPE_THETA ** (-jnp.arange(half, dtype=F32) / half)
    ang = pos[:, None] * inv_freq[None, :]
    cos2 = jnp.concatenate([jnp.cos(ang)] * 2, axis=-1)
    sin2 = jnp.concatenate([jnp.sin(ang)] * 2, axis=-1)
    return cos2, sin2, cos2.T, sin2.T


def _trunk(x, mem, layers, final_norm):
    b, s, d = x.shape
    nm = mem.shape[1]
    rope = _rope_tables(s)
    x2 = x.reshape(b * s, d)
    mem2 = mem.reshape(b * nm, d)
    for w in layers:
        qt, k, vt = mla_prep(x, w["mix_norm"], w["wm"], w["q_norm"], w["wqt"], w["kv_norm"],
                             w["wk"], w["wvt"], *rope)
        a_out = mla_attention(qt, k, vt).reshape(b * s, MLA_HEADS * MLA_V)
        proj_hg = norm_matmul(x2, w["mix_norm"], w["w_hg"])
        o_f = hgrn_direction(proj_hg, w["lb_rows"][0], batch=b, seq=s,
                             q_blk=0, f_blk=1, v_blk=3, reverse=False)
        o_b = hgrn_direction(proj_hg, w["lb_rows"][1], batch=b, seq=s,
                             q_blk=0, f_blk=2, v_blk=3, reverse=True)
        x2 = out_projection(a_out, o_f.reshape(b * s, HG_WIDTH), o_b.reshape(b * s, HG_WIDTH),
                            proj_hg, 4, w["out_norm"], w["w_out"], x2)
        kmem = norm_matmul(mem2, w["mem_norm"], w["w_xk"], out_dtype=BF16).reshape(b, nm, d)
        vmem = norm_matmul(mem2, w["mem_norm"], w["w_xv"], out_dtype=BF16).reshape(b, nm, d)
        xo = cross_attention(x2, w["xattn_norm"], w["w_xq"], kmem, vmem, seq=s)
        x2 = matmul_residual(xo, w["w_xo"], x2)
        u = norm_matmul(x2, w["ffn_norm"], w["w_ffn1"], act="relu2", out_dtype=BF16)
        x2 = matmul_residual(u, w["w_ffn2"], x2)
        x = x2.reshape(b, s, d)
    return rmsnorm_rows(x2, final_norm).reshape(b, s, d)


def kernel(x_prompt, x_sample, mem_prompt, mem_sample, mix_norm, w_in, q_norm, w_uq, kv_norm, w_ukv, hgrn_lb_logits, hgrn_out_norm, w_out, xattn_norm, mem_norm, w_xq, w_xk, w_xv, w_xo, ffn_norm, w_ffn1, w_ffn2, final_norm):
    depth = w_in.shape[0]
    lb_all = jnp.cumsum(jax.nn.softmax(hgrn_lb_logits.astype(F32), axis=0), axis=0)
    lb_all = lb_all - lb_all[0:1]
    layers = [
        _layer_weights(l, mix_norm, w_in, q_norm, w_uq, kv_norm, w_ukv, lb_all, hgrn_out_norm,
                       w_out, xattn_norm, mem_norm, w_xq, w_xk, w_xv, w_xo, ffn_norm, w_ffn1,
                       w_ffn2)
        for l in range(depth)
    ]
    y_prompt = _trunk(x_prompt, mem_prompt, layers, final_norm)
    y_sample = _trunk(x_sample, mem_sample, layers, final_norm)
    return (y_prompt, y_sample)
```
